```python
import jax, jax.numpy as jnp
from jax import lax
import numpy as np

D_MODEL = 1024
BATCH = 4
SEQ = 8192
DEPTH = 2

N_MIXERS = 2
D_FF = 2816
CONV_WIDTH = 3
MLSTM_HEADS = 4
MLSTM_QK_DIM = D_MODEL // 2 // MLSTM_HEADS
MLSTM_V_DIM = D_MODEL // MLSTM_HEADS
MLSTM_CHUNK = 64
MLSTM_IN_DIM = 2 * MLSTM_HEADS * MLSTM_QK_DIM + 2 * MLSTM_HEADS * MLSTM_V_DIM + 2 * MLSTM_HEADS
NORM_EPS = 1e-6

kernel_name = "hybrid_shortconv_mlstm_macaron"


def rms_norm(x, g):
    xf = x.astype(jnp.float32)
    y = xf * lax.rsqrt(jnp.mean(xf * xf, axis=-1, keepdims=True) + NORM_EPS)
    return (y * g.astype(jnp.float32)).astype(x.dtype)


def swiglu(x, w_gate, w_up, w_down):
    return (jax.nn.silu(x @ w_gate) * (x @ w_up)) @ w_down


def short_conv_mixer(x, w_in, conv_w, w_out):
    S = x.shape[1]
    gate_b, gate_c, h = jnp.split(x @ w_in, 3, axis=-1)
    u = gate_c * h
    up = jnp.pad(u, ((0, 0), (CONV_WIDTH - 1, 0), (0, 0)))
    conv = conv_w[0] * up[:, 0:S]
    for k in range(1, CONV_WIDTH):
        conv = conv + conv_w[k] * up[:, k:k + S]
    return (gate_b * conv) @ w_out


def mlstm_chunkwise(q, k, v, log_i, log_f):
    B, H, S, DQK = q.shape
    DV = v.shape[-1]
    L = MLSTM_CHUNK
    NC = S // L

    def to_chunks(t):
        return jnp.moveaxis(t.reshape(B, H, NC, L, *t.shape[3:]), 2, 0)

    causal = jnp.tril(jnp.ones((L, L), dtype=bool))

    def step(carry, inp):
        C, n, m = carry
        qj, kj, vj, ij, fj = inp
        bcum = jnp.cumsum(fj, axis=-1)
        log_d = bcum[..., :, None] - bcum[..., None, :] + ij[..., None, :]
        log_d = jnp.where(causal, log_d, -jnp.inf)
        log_inter = bcum + m[..., None]
        m_row = jnp.maximum(log_inter, jnp.max(log_d, axis=-1))
        w_intra = jnp.exp(log_d - m_row[..., None])
        w_inter = jnp.exp(log_inter - m_row)
        s = jnp.einsum('bhld,bhsd->bhls', qj, kj) * w_intra
        num = (jnp.einsum('bhls,bhsv->bhlv', s, vj)
               + w_inter[..., None] * jnp.einsum('bhld,bhdv->bhlv', qj, C))
        den = jnp.sum(s, axis=-1) + w_inter * jnp.einsum('bhld,bhd->bhl', qj, n)
        h = num / jnp.maximum(jnp.abs(den), jnp.exp(-m_row))[..., None]
        b_last = bcum[..., -1]
        log_w = b_last[..., None] - bcum + ij
        m_new = jnp.maximum(b_last + m, jnp.max(log_w, axis=-1))
        w_state = jnp.exp(log_w - m_new[..., None])
        decay = jnp.exp(b_last + m - m_new)
        C_new = decay[..., None, None] * C + jnp.einsum('bhs,bhsd,bhsv->bhdv', w_state, kj, vj)
        n_new = decay[..., None] * n + jnp.einsum('bhs,bhsd->bhd', w_state, kj)
        return (C_new, n_new, m_new), h

    init = (jnp.zeros((B, H, DQK, DV), jnp.float32),
            jnp.zeros((B, H, DQK), jnp.float32),
            jnp.zeros((B, H), jnp.float32))
    _, hc = lax.scan(step, init, tuple(to_chunks(t) for t in (q, k, v, log_i, log_f)))
    return jnp.moveaxis(hc, 0, 2).reshape(B, H, S, DV)


def mlstm_mixer(x, w_in, b_gates, head_norm, w_out):
    B, S, _ = x.shape
    NH, DQK, DV = MLSTM_HEADS, MLSTM_QK_DIM, MLSTM_V_DIM
    splits = [NH * DQK, 2 * NH * DQK, 2 * NH * DQK + NH * DV, 2 * NH * DQK + 2 * NH * DV]
    q, k, v, o, g = jnp.split(x @ w_in, splits, axis=-1)

    def heads(t, d):
        return t.reshape(B, S, NH, d).transpose(0, 2, 1, 3).astype(jnp.float32)

    q = heads(q, DQK)
    k = heads(k, DQK) * (DQK ** -0.5)
    v = heads(v, DV)
    g = (g.astype(jnp.float32) + b_gates.astype(jnp.float32)).transpose(0, 2, 1)
    log_i = g[:, :NH]
    log_f = jax.nn.log_sigmoid(g[:, NH:])
    h = mlstm_chunkwise(q, k, v, log_i, log_f)
    h = h * lax.rsqrt(jnp.mean(h * h, axis=-1, keepdims=True) + NORM_EPS)
    h = h * head_norm.astype(jnp.float32)[None, :, None, :]
    h = h.transpose(0, 2, 1, 3).reshape(B, S, NH * DV).astype(x.dtype)
    return (h * jax.nn.sigmoid(o)) @ w_out


def setup_inputs(seed: int = 0) -> dict:
    key = jax.random.key(seed)
    ks = jax.random.split(key, 20)
    D, F = D_MODEL, D_FF
    n_conv = (DEPTH + 1) // 2
    n_ml = DEPTH // 2
    nrm = lambda k, shape, scale: jax.random.normal(k, shape, jnp.float32) * scale
    forget_bias = jnp.linspace(3.0, 6.0, MLSTM_HEADS, dtype=jnp.float32)
    b_gates = jnp.concatenate([
        nrm(ks[10], (n_ml, MLSTM_HEADS), 0.1),
        forget_bias[None, :] + nrm(ks[11], (n_ml, MLSTM_HEADS), 0.1)], axis=-1)
    return {
        "x": nrm(ks[0], (BATCH, SEQ, D), 1.0),
        "norm_g": 1.0 + nrm(ks[1], (DEPTH, 3, D), 0.02),
        "ffn_w_gate": nrm(ks[2], (DEPTH, 2, D, F), D ** -0.5),
        "ffn_w_up": nrm(ks[3], (DEPTH, 2, D, F), D ** -0.5),
        "ffn_w_down": nrm(ks[4], (DEPTH, 2, F, D), F ** -0.5),
        "conv_w_in": nrm(ks[5], (n_conv, D, 3 * D), D ** -0.5),
        "conv_w": nrm(ks[6], (n_conv, CONV_WIDTH, D), CONV_WIDTH ** -0.5),
        "conv_w_out": nrm(ks[7], (n_conv, D, D), D ** -0.5),
        "mlstm_w_in": nrm(ks[8], (n_ml, D, MLSTM_IN_DIM), D ** -0.5),
        "mlstm_b_gates": b_gates,
        "mlstm_head_norm": 1.0 + nrm(ks[12], (n_ml, MLSTM_HEADS, MLSTM_V_DIM), 0.02),
        "mlstm_w_out": nrm(ks[9], (n_ml, D, D), D ** -0.5),
        "final_norm_g": 1.0 + nrm(ks[13], (D,), 0.02),
    }


def reference(x, norm_g, ffn_w_gate, ffn_w_up, ffn_w_down, conv_w_in, conv_w, conv_w_out,
              mlstm_w_in, mlstm_b_gates, mlstm_head_norm, mlstm_w_out, final_norm_g):
    h = x
    for layer in range(DEPTH):
        h = h + 0.5 * swiglu(rms_norm(h, norm_g[layer, 0]),
                             ffn_w_gate[layer, 0], ffn_w_up[layer, 0], ffn_w_down[layer, 0])
        hn = rms_norm(h, norm_g[layer, 1])
        j = layer // N_MIXERS
        if layer % N_MIXERS == 0:
            mix = short_conv_mixer(hn, conv_w_in[j], conv_w[j], conv_w_out[j])
        else:
            mix = mlstm_mixer(hn, mlstm_w_in[j], mlstm_b_gates[j], mlstm_head_norm[j], mlstm_w_out[j])
        h = h + mix
        h = h + 0.5 * swiglu(rms_norm(h, norm_g[layer, 2]),
                             ffn_w_gate[layer, 1], ffn_w_up[layer, 1], ffn_w_down[layer, 1])
    return rms_norm(h, final_norm_g)
```

```python
import functools

import jax
import jax.numpy as jnp
from jax import lax
from jax.experimental import pallas as pl
from jax.experimental.pallas import tpu as pltpu

NORM_EPS = 1e-6
CONV_WIDTH = 3
MLSTM_HEADS = 4
NEG = -1e30

V7X_LANES = 128
V7X_SUBLANES = 8
V7X_VMEM_BYTES = 64 * 1024 * 1024
VMEM_LIMIT = 56 * 1024 * 1024

F32 = jnp.float32
BF16 = jnp.bfloat16


def _rms(x, g):
    return x * lax.rsqrt(jnp.mean(x * x, axis=-1, keepdims=True) + NORM_EPS) * g


def _sigmoid(x):
    return 1.0 / (1.0 + jnp.exp(-x))


def _params(sem):
    return pltpu.CompilerParams(dimension_semantics=sem, vmem_limit_bytes=VMEM_LIMIT)


def _resident(shape):
    nd = len(shape)
    return pl.BlockSpec(shape, lambda *_: (0,) * nd, pipeline_mode=pl.Buffered(1))


def _ffn_kernel(h_ref, g_ref, wg_ref, wu_ref, wd_ref, fg_ref, o_ref, acc_ref, *, fc, final):
    h = h_ref[...]
    xn = _rms(h, g_ref[...]).astype(BF16)
    n_chunks = wg_ref.shape[1] // fc
    for c in range(n_chunks):
        cols = slice(c * fc, (c + 1) * fc)
        gate = jnp.dot(xn, wg_ref[:, cols], preferred_element_type=F32)
        up = jnp.dot(xn, wu_ref[:, cols], preferred_element_type=F32)
        act = (gate * _sigmoid(gate) * up).astype(BF16)
        part = jnp.dot(act, wd_ref[cols, :], preferred_element_type=F32)
        if c == 0:
            acc_ref[...] = part
        else:
            acc_ref[...] += part
    out = h + 0.5 * acc_ref[...]
    if final:
        out = _rms(out, fg_ref[...])
    o_ref[...] = out


def _ffn(h, g, wg, wu, wd, fg, *, tm, fc, final):
    t, d = h.shape
    f = wg.shape[1]
    return pl.pallas_call(
        functools.partial(_ffn_kernel, fc=fc, final=final),
        grid=(t // tm,),
        in_specs=[
            pl.BlockSpec((tm, d), lambda i: (i, 0)),
            _resident((1, d)),
            _resident((d, f)),
            _resident((d, f)),
            _resident((f, d)),
            _resident((1, d)),
        ],
        out_specs=pl.BlockSpec((tm, d), lambda i: (i, 0)),
        out_shape=jax.ShapeDtypeStruct((t, d), F32),
        scratch_shapes=[pltpu.VMEM((tm, d), F32)],
        compiler_params=_params(("parallel",)),
        name="ffn_final" if final else "ffn",
    )(h, g, wg, wu, wd, fg)


def _conv_kernel(h_ref, g_ref, win_ref, cw_ref, wout_ref, o_ref, u_ref):
    tm, d = h_ref.shape[1], h_ref.shape[2]
    pad = V7X_SUBLANES

    @pl.when(pl.program_id(1) == 0)
    def _():
        u_ref[0:pad, :] = jnp.zeros((pad, d), F32)

    h = h_ref[0]
    xn = _rms(h, g_ref[...]).astype(BF16)
    gate_b = jnp.dot(xn, win_ref[:, 0:d], preferred_element_type=F32)
    gate_c = jnp.dot(xn, win_ref[:, d:2 * d], preferred_element_type=F32)
    val = jnp.dot(xn, win_ref[:, 2 * d:3 * d], preferred_element_type=F32)
    u = gate_c * val
    u_ref[pad:pad + tm, :] = u
    conv = cw_ref[CONV_WIDTH - 1:CONV_WIDTH, :] * u
    for k in range(CONV_WIDTH - 1):
        shift = CONV_WIDTH - 1 - k
        conv = conv + cw_ref[k:k + 1, :] * u_ref[pad - shift:pad - shift + tm, :]
    y = (gate_b * conv).astype(BF16)
    o_ref[0] = h + jnp.dot(y, wout_ref[...], preferred_element_type=F32)
    u_ref[0:pad, :] = u_ref[tm:tm + pad, :]


def _conv_mixer(h, g, win, cw, wout, *, tm):
    b, s, d = h.shape
    return pl.pallas_call(
        _conv_kernel,
        grid=(b, s // tm),
        in_specs=[
            pl.BlockSpec((1, tm, d), lambda i, j: (i, j, 0)),
            _resident((1, d)),
            _resident((d, 3 * d)),
            _resident((CONV_WIDTH, d)),
            _resident((d, d)),
        ],
        out_specs=pl.BlockSpec((1, tm, d), lambda i, j: (i, j, 0)),
        out_shape=jax.ShapeDtypeStruct((b, s, d), F32),
        scratch_shapes=[pltpu.VMEM((tm + V7X_SUBLANES, d), F32)],
        compiler_params=_params(("arbitrary", "arbitrary")),
        name="conv_mixer",
    )(h, g, win, cw, wout)


def _mlstm_in_kernel(h_ref, g_ref, wq_ref, wv_ref, wkg_ref, q_ref, kt_ref, va_ref, gi_ref, gf_ref,
                     *, dqk, dv, k_scale):
    nh = MLSTM_HEADS
    tm = h_ref.shape[1]
    xn = _rms(h_ref[0], g_ref[...]).astype(BF16)
    q_ref[0] = jnp.dot(xn, wq_ref[...], preferred_element_type=F32).astype(BF16)
    v = jnp.dot(xn, wv_ref[...], preferred_element_type=F32).astype(BF16)
    dva = dv + V7X_LANES
    ones = jnp.ones((tm, V7X_LANES), BF16)
    for hd in range(nh):
        va_ref[0, :, hd * dva:hd * dva + dv] = v[:, hd * dv:(hd + 1) * dv]
        va_ref[0, :, hd * dva + dv:(hd + 1) * dva] = ones
    kg = lax.dot_general(wkg_ref[...], xn, (((1,), (1,)), ((), ())), preferred_element_type=F32)
    kt_ref[0] = (kg[0:nh * dqk, :] * k_scale).astype(BF16)
    gi_ref[0] = kg[nh * dqk:nh * dqk + V7X_SUBLANES, :]
    gf_ref[0] = kg[nh * dqk + V7X_SUBLANES:nh * dqk + 2 * V7X_SUBLANES, :]


def _mlstm_in(h, g, wq, wv, wkg, *, tm, dqk, dv):
    b, s, d = h.shape
    nh = MLSTM_HEADS
    dva = dv + V7X_LANES
    kern = functools.partial(_mlstm_in_kernel, dqk=dqk, dv=dv, k_scale=dqk ** -0.5)
    return pl.pallas_call(
        kern,
        grid=(b, s // tm),
        in_specs=[
            pl.BlockSpec((1, tm, d), lambda i, j: (i, j, 0)),
            _resident((1, d)),
            _resident(wq.shape),
            _resident(wv.shape),
            _resident(wkg.shape),
        ],
        out_specs=[
            pl.BlockSpec((1, tm, nh * dqk), lambda i, j: (i, j, 0)),
            pl.BlockSpec((1, nh * dqk, tm), lambda i, j: (i, 0, j)),
            pl.BlockSpec((1, tm, nh * dva), lambda i, j: (i, j, 0)),
            pl.BlockSpec((1, V7X_SUBLANES, tm), lambda i, j: (i, 0, j)),
            pl.BlockSpec((1, V7X_SUBLANES, tm), lambda i, j: (i, 0, j)),
        ],
        out_shape=[
            jax.ShapeDtypeStruct((b, s, nh * dqk), BF16),
            jax.ShapeDtypeStruct((b, nh * dqk, s), BF16),
            jax.ShapeDtypeStruct((b, s, nh * dva), BF16),
            jax.ShapeDtypeStruct((b, V7X_SUBLANES, s), F32),
            jax.ShapeDtypeStruct((b, V7X_SUBLANES, s), F32),
        ],
        compiler_params=_params(("parallel", "parallel")),
        name="mlstm_in",
    )(h, g, wq, wv, wkg)


def _gate_kernel(gi_ref, gf_ref, bi_ref, bf_ref, a_ref, wst_ref, dec_ref, m_ref, wint_ref, clamp_ref,
                 *, chunk):
    s = gi_ref.shape[-1]
    gi = gi_ref[0] + bi_ref[:, 0:1]
    gf = gf_ref[0] + bf_ref[:, 0:1]
    logf = jnp.minimum(gf, 0.0) - jnp.log1p(jnp.exp(-jnp.abs(gf)))
    t = lax.broadcasted_iota(jnp.int32, gi.shape, 1)
    tl = jnp.bitwise_and(t, chunk - 1)
    last = tl == chunk - 1

    def fwd(x, k):
        return pltpu.roll(x, k, 1)

    def bwd(x, k):
        return pltpu.roll(x, s - k, 1)

    bc = logf
    k = 1
    while k < chunk:
        bc = bc + jnp.where(tl >= k, fwd(bc, k), 0.0)
        k *= 2

    bs, cs = logf, gi
    k = 1
    while k < s:
        bp = jnp.where(t >= k, fwd(bs, k), 0.0)
        cp = jnp.where(t >= k, fwd(cs, k), NEG)
        cs = jnp.maximum(cp + bs, cs)
        bs = bp + bs
        k *= 2
    m = jnp.maximum(bs, cs)

    def spread_fwd(x):
        k = 1
        while k < chunk:
            x = jnp.maximum(x, jnp.where(tl >= k, fwd(x, k), NEG))
            k *= 2
        return x

    def spread_bwd(x):
        x = jnp.where(last, x, NEG)
        k = 1
        while k < chunk:
            x = jnp.maximum(x, jnp.where(tl < chunk - k, bwd(x, k), NEG))
            k *= 2
        return x

    m_prev = jnp.where(tl == 0, jnp.where(t == 0, 0.0, fwd(m, 1)), NEG)
    m_prev = spread_fwd(m_prev)
    big_m = m - bc
    a = gi - bc
    m_end = spread_bwd(m)
    bc_end = spread_bwd(bc)
    a_ref[0] = a
    wst_ref[0] = jnp.exp(a - (m_end - bc_end))
    dec_ref[0] = jnp.exp(bc_end + m_prev - m_end)
    m_ref[0] = big_m
    wint_ref[0] = jnp.exp(m_prev - big_m)
    clamp_ref[0] = jnp.exp(-m)


def _mlstm_gates(gi, gf, bi, bf, *, chunk):
    b, r, s = gi.shape
    spec = pl.BlockSpec((1, r, s), lambda i: (i, 0, 0))
    bspec = pl.BlockSpec((r, V7X_LANES), lambda i: (0, 0))
    return pl.pallas_call(
        functools.partial(_gate_kernel, chunk=chunk),
        grid=(b,),
        in_specs=[spec, spec, bspec, bspec],
        out_specs=[spec] * 6,
        out_shape=[jax.ShapeDtypeStruct((b, r, s), F32)] * 6,
        compiler_params=_params(("parallel",)),
        name="mlstm_gates",
    )(gi, gf, bi, bf)


def _mlstm_rec_kernel(q_ref, kt_ref, va_ref, a_ref, wst_ref, dec_ref, m_ref, wint_ref, clamp_ref, hw_ref,
                      o_ref, c_ref, *, chunk, dqk, dv):
    nh = MLSTM_HEADS
    ts = q_ref.shape[1]
    dva = dv + V7X_LANES

    @pl.when(pl.program_id(1) == 0)
    def _():
        c_ref[...] = jnp.zeros(c_ref.shape, F32)

    row = lax.broadcasted_iota(jnp.int32, (chunk, chunk), 0)
    col = lax.broadcasted_iota(jnp.int32, (chunk, chunk), 1)
    causal = col <= row

    for c in range(ts // chunk):
        tok = slice(c * chunk, (c + 1) * chunk)
        for hd in range(nh):
            q = q_ref[0, tok, hd * dqk:(hd + 1) * dqk]
            kt = kt_ref[0, hd * dqk:(hd + 1) * dqk, tok]
            va = va_ref[0, tok, hd * dva:(hd + 1) * dva]
            a_row = a_ref[0, hd:hd + 1, tok]
            wst_row = wst_ref[0, hd:hd + 1, tok]
            dec = dec_ref[0, hd:hd + 1, c * chunk:c * chunk + 1]
            m_col = m_ref[0, tok, hd:hd + 1]
            wint_col = wint_ref[0, tok, hd:hd + 1]
            clamp_col = clamp_ref[0, tok, hd:hd + 1]

            p = jnp.exp(jnp.where(causal, a_row - m_col, NEG))
            sc = (jnp.dot(q, kt, preferred_element_type=F32) * p).astype(BF16)
            state = c_ref[hd]
            nd = (jnp.dot(sc, va, preferred_element_type=F32)
                  + wint_col * jnp.dot(q, state.astype(BF16), preferred_element_type=F32))
            den = nd[:, dv:dv + V7X_LANES]
            inv = 1.0 / jnp.maximum(jnp.abs(den), clamp_col)
            hh = nd[:, 0:dv] * jnp.concatenate([inv] * (dv // V7X_LANES), axis=1)
            hh = hh * lax.rsqrt(jnp.mean(hh * hh, axis=-1, keepdims=True) + NORM_EPS)
            o_ref[0, tok, hd * dv:(hd + 1) * dv] = (hh * hw_ref[:, hd * dv:(hd + 1) * dv]).astype(BF16)

            ktw = (kt.astype(F32) * wst_row).astype(BF16)
            c_ref[hd] = dec * state + jnp.dot(ktw, va, preferred_element_type=F32)


def _mlstm_rec(q, kt, va, a, wst, dec, m_col, wint_col, clamp_col, hw, *, ts, chunk, dqk, dv):
    b, s, _ = q.shape
    nh = MLSTM_HEADS
    dva = dv + V7X_LANES
    r = a.shape[1]
    tok_major = lambda w: pl.BlockSpec((1, ts, w), lambda i, j: (i, j, 0))
    feat_major = lambda w: pl.BlockSpec((1, w, ts), lambda i, j: (i, 0, j))
    return pl.pallas_call(
        functools.partial(_mlstm_rec_kernel, chunk=chunk, dqk=dqk, dv=dv),
        grid=(b, s // ts),
        in_specs=[
            tok_major(nh * dqk), feat_major(nh * dqk), tok_major(nh * dva),
            feat_major(r), feat_major(r), feat_major(r),
            tok_major(r), tok_major(r), tok_major(r),
            _resident((1, nh * dv)),
        ],
        out_specs=tok_major(nh * dv),
        out_shape=jax.ShapeDtypeStruct((b, s, nh * dv), BF16),
        scratch_shapes=[pltpu.VMEM((nh, dqk, dva), F32)],
        compiler_params=_params(("arbitrary", "arbitrary")),
        name="mlstm_rec",
    )(q, kt, va, a, wst, dec, m_col, wint_col, clamp_col, hw)


def _mlstm_out_kernel(h_ref, hn_ref, g_ref, wo_ref, wout_ref, o_ref):
    h = h_ref[...]
    xn = _rms(h, g_ref[...]).astype(BF16)
    og = jnp.dot(xn, wo_ref[...], preferred_element_type=F32)
    y = (hn_ref[...].astype(F32) * _sigmoid(og)).astype(BF16)
    o_ref[...] = h + jnp.dot(y, wout_ref[...], preferred_element_type=F32)


def _mlstm_out(h, hn, g, wo, wout, *, tm):
    t, d = h.shape
    e = hn.shape[1]
    return pl.pallas_call(
        _mlstm_out_kernel,
        grid=(t // tm,),
        in_specs=[
            pl.BlockSpec((tm, d), lambda i: (i, 0)),
            pl.BlockSpec((tm, e), lambda i: (i, 0)),
            _resident((1, d)),
            _resident((d, e)),
            _resident((e, d)),
        ],
        out_specs=pl.BlockSpec((tm, d), lambda i: (i, 0)),
        out_shape=jax.ShapeDtypeStruct((t, d), F32),
        compiler_params=_params(("parallel",)),
        name="mlstm_out",
    )(h, hn, g, wo, wout)


def _tiles(b, s):
    tm = min(512, s)
    return dict(tm=tm, ts=min(512, s), chunk=min(128, s))


def kernel(x, norm_g, ffn_w_gate, ffn_w_up, ffn_w_down, conv_w_in, conv_w, conv_w_out, mlstm_w_in,
           mlstm_b_gates, mlstm_head_norm, mlstm_w_out, final_norm_g):
    b, s, d = x.shape
    depth = norm_g.shape[0]
    f = ffn_w_gate.shape[-1]
    nh = MLSTM_HEADS
    dv = mlstm_head_norm.shape[-1]
    dqk = (mlstm_w_in.shape[-1] - 2 * nh * dv - 2 * nh) // (2 * nh)
    cfg = _tiles(b, s)
    tm, ts, chunk = cfg["tm"], cfg["ts"], cfg["chunk"]
    fc = 256 if f % 256 == 0 else f
    assert s % tm == 0 and s % ts == 0 and ts % chunk == 0 and chunk & (chunk - 1) == 0
    assert s & (s - 1) == 0, "gate scan assumes a power-of-two sequence length"

    row = lambda v: v.reshape(1, -1).astype(F32)
    fg = row(final_norm_g)
    h = x.astype(F32)

    def ffn(h, layer, k, final=False):
        out = _ffn(h.reshape(b * s, d), row(norm_g[layer, 0 if k == 0 else 2]),
                   ffn_w_gate[layer, k].astype(BF16), ffn_w_up[layer, k].astype(BF16),
                   ffn_w_down[layer, k].astype(BF16), fg, tm=tm, fc=fc, final=final)
        return out.reshape(b, s, d)

    for layer in range(depth):
        h = ffn(h, layer, 0)
        j = layer // 2
        g_mix = row(norm_g[layer, 1])
        if layer % 2 == 0:
            h = _conv_mixer(h, g_mix, conv_w_in[j].astype(BF16), conv_w[j].astype(F32),
                            conv_w_out[j].astype(BF16), tm=tm)
        else:
            w = mlstm_w_in[j]
            o_q, o_k, o_v, o_o = 0, nh * dqk, 2 * nh * dqk, 2 * nh * dqk + nh * dv
            o_g = o_o + nh * dv
            wq = w[:, o_q:o_k].astype(BF16)
            wv = w[:, o_v:o_o].astype(BF16)
            wo = w[:, o_o:o_g].astype(BF16)
            pad = jnp.zeros((d, V7X_SUBLANES - nh), w.dtype)
            wkg = jnp.concatenate([w[:, o_k:o_v], w[:, o_g:o_g + nh], pad, w[:, o_g + nh:], pad],
                                  axis=1).T.astype(BF16)
            bias = mlstm_b_gates[j].astype(F32)
            zpad = jnp.zeros((V7X_SUBLANES - nh,), F32)
            bi = jnp.broadcast_to(jnp.concatenate([bias[:nh], zpad])[:, None], (V7X_SUBLANES, V7X_LANES))
            bf = jnp.broadcast_to(jnp.concatenate([bias[nh:], zpad])[:, None], (V7X_SUBLANES, V7X_LANES))

            q, kt, va, gi, gf = _mlstm_in(h, g_mix, wq, wv, wkg, tm=tm, dqk=dqk, dv=dv)
            a, wst, dec, big_m, wint, clamp = _mlstm_gates(gi, gf, bi, bf, chunk=chunk)
            tr = lambda v: jnp.swapaxes(v, 1, 2)
            hn = _mlstm_rec(q, kt, va, a, wst, dec, tr(big_m), tr(wint), tr(clamp),
                            row(mlstm_head_norm[j]), ts=ts, chunk=chunk, dqk=dqk, dv=dv)
            h = _mlstm_out(h.reshape(b * s, d), hn.reshape(b * s, nh * dv), g_mix, wo,
                           mlstm_w_out[j].astype(BF16), tm=tm).reshape(b, s, d)
        h = ffn(h, layer, 1, final=(layer == depth - 1))
    return h.astype(x.dtype)
```
